```python
import math
import jax, jax.numpy as jnp
from jax import lax
import numpy as np

D_MODEL = 2048
BATCH = 1
SEQ = 8192
DEPTH = 1

ATT_HEADS = 16
ATT_KV_HEADS = 2
HEAD_DIM = 64
ATT_WIDTH = ATT_HEADS * HEAD_DIM
KV_WIDTH = ATT_KV_HEADS * HEAD_DIM
WINDOW = 128
ATT_BLOCK = 128
ROT_DIM = HEAD_DIM // 4
ROPE_THETA = 500000.0
SSD_HEADS = 16
SSD_HEAD_DIM = 64
SSD_WIDTH = SSD_HEADS * SSD_HEAD_DIM
SSD_GROUPS = 2
SSD_STATE = 128
CONV_WIDTH = 4
SSD_CHUNK = 128
XBC_WIDTH = SSD_WIDTH + 2 * SSD_GROUPS * SSD_STATE
MIX_WIDTH = ATT_WIDTH + SSD_WIDTH
OFF_Q = ATT_WIDTH
OFF_K = OFF_Q + KV_WIDTH
OFF_V = OFF_K + KV_WIDTH
OFF_Z = OFF_V + SSD_WIDTH
OFF_XBC = OFF_Z + XBC_WIDTH
IN_WIDTH = OFF_XBC + SSD_HEADS
N_KEYS = 128
N_EXPERTS = N_KEYS * N_KEYS
PEER_HEADS = 8
PEER_KEY_DIM = 128
PEER_TOPK = 16
PEER_BLOCK = 128
ALPHA = (2.0 * DEPTH) ** 0.25
BETA = (8.0 * DEPTH) ** -0.25
LN_EPS = 1e-5
N_MOD = 6

kernel_name = "hymba_swa_ssd_peer_deepnorm_adaln"


def layer_norm(x, g, b):
    xf = x.astype(jnp.float32)
    mu = jnp.mean(xf, -1, keepdims=True)
    var = jnp.mean(jnp.square(xf - mu), -1, keepdims=True)
    return ((xf - mu) * lax.rsqrt(var + LN_EPS) * g.astype(jnp.float32) + b.astype(jnp.float32)).astype(x.dtype)


def rope_tables(positions):
    inv = ROPE_THETA ** (-jnp.arange(0, ROT_DIM, 2, dtype=jnp.float32) / ROT_DIM)
    ang = positions.astype(jnp.float32)[..., None] * inv
    return jnp.cos(ang), jnp.sin(ang)


def apply_partial_rope(t, cos, sin):
    half = ROT_DIM // 2
    t1 = t[..., :half].astype(jnp.float32)
    t2 = t[..., half:ROT_DIM].astype(jnp.float32)
    rot = jnp.concatenate([t1 * cos - t2 * sin, t2 * cos + t1 * sin], -1).astype(t.dtype)
    return jnp.concatenate([rot, t[..., ROT_DIM:]], -1)


def sliding_window_attention(q, k, v, sinks):
    B, S = q.shape[0], q.shape[1]
    nb = S // ATT_BLOCK
    G = ATT_HEADS // ATT_KV_HEADS
    qb = q.reshape(B, nb, ATT_BLOCK, ATT_KV_HEADS, G, HEAD_DIM)

    def band(t):
        tb = t.reshape(B, nb, ATT_BLOCK, ATT_KV_HEADS, HEAD_DIM)
        prev = jnp.pad(tb[:, :-1], ((0, 0), (1, 0), (0, 0), (0, 0), (0, 0)))
        return jnp.concatenate([prev, tb], axis=2)

    kb, vb = band(k), band(v)
    logits = jnp.einsum('bnqhgd,bnkhd->bnhgqk', qb, kb,
                        preferred_element_type=jnp.float32) * (HEAD_DIM ** -0.5)
    blk = jnp.arange(nb)[:, None, None] * ATT_BLOCK
    qpos = blk + jnp.arange(ATT_BLOCK)[None, :, None]
    kpos = blk - ATT_BLOCK + jnp.arange(2 * ATT_BLOCK)[None, None, :]
    rel = qpos - kpos
    valid = (rel >= 0) & (rel < WINDOW) & (kpos >= 0)
    logits = jnp.where(valid[None, :, None, None], logits, -jnp.inf)
    sink = sinks.astype(jnp.float32).reshape(1, 1, ATT_KV_HEADS, G, 1, 1)
    m = jnp.maximum(jnp.max(logits, -1, keepdims=True), sink)
    p = jnp.exp(logits - m)
    probs = (p / (jnp.sum(p, -1, keepdims=True) + jnp.exp(sink - m))).astype(v.dtype)
    out = jnp.einsum('bnhgqk,bnkhd->bnqhgd', probs, vb)
    return out.reshape(B, S, ATT_WIDTH)


def causal_depthwise_conv(u, w, b):
    out = lax.conv_general_dilated(u, w[:, None, :], window_strides=(1,),
                                   padding=[(CONV_WIDTH - 1, 0)],
                                   dimension_numbers=('NWC', 'WIO', 'NWC'),
                                   feature_group_count=u.shape[-1])
    return out + b


def ssd_chunked(xs, dt, A, Bm, Cm):
    Bsz, S = xs.shape[0], xs.shape[1]
    nc = S // SSD_CHUNK
    R = SSD_HEADS // SSD_GROUPS
    x = xs.reshape(Bsz, nc, SSD_CHUNK, SSD_GROUPS, R, SSD_HEAD_DIM)
    dtc = dt.reshape(Bsz, nc, SSD_CHUNK, SSD_GROUPS, R)
    Bc = Bm.reshape(Bsz, nc, SSD_CHUNK, SSD_GROUPS, SSD_STATE)
    Cc = Cm.reshape(Bsz, nc, SSD_CHUNK, SSD_GROUPS, SSD_STATE)
    a_cum = jnp.cumsum(dtc * A.reshape(SSD_GROUPS, R), axis=2)
    xdt = x * dtc[..., None]
    seg = a_cum[:, :, :, None] - a_cum[:, :, None, :]
    causal = jnp.tril(jnp.ones((SSD_CHUNK, SSD_CHUNK), dtype=bool))[:, :, None, None]
    Lmat = jnp.exp(jnp.where(causal, seg, -jnp.inf))
    cb = jnp.einsum('bclgn,bcsgn->bclsg', Cc, Bc)
    y_diag = jnp.einsum('bclsg,bclsgr,bcsgrp->bclgrp', cb, Lmat, xdt)
    decay_to_end = jnp.exp(a_cum[:, :, -1:] - a_cum)
    chunk_states = jnp.einsum('bclgn,bclgr,bclgrp->bcgrpn', Bc, decay_to_end, xdt)
    chunk_decay = jnp.exp(a_cum[:, :, -1])

    def step(h, inp):
        dec, st = inp
        return dec[..., None, None] * h + st, h

    h0 = jnp.zeros((Bsz, SSD_GROUPS, R, SSD_HEAD_DIM, SSD_STATE), jnp.float32)
    _, h_in = lax.scan(step, h0, (jnp.moveaxis(chunk_decay, 1, 0), jnp.moveaxis(chunk_states, 1, 0)))
    h_in = jnp.moveaxis(h_in, 0, 1)
    y_off = jnp.einsum('bclgn,bcgrpn,bclgr->bclgrp', Cc, h_in, jnp.exp(a_cum))
    return (y_diag + y_off).reshape(Bsz, S, SSD_WIDTH)


def gated_rmsnorm(y, z, w):
    yf = y.astype(jnp.float32) * jax.nn.silu(z.astype(jnp.float32))
    yg = yf.reshape(yf.shape[:-1] + (SSD_GROUPS, SSD_WIDTH // SSD_GROUPS))
    yg = yg * lax.rsqrt(jnp.mean(jnp.square(yg), -1, keepdims=True) + LN_EPS)
    return yg.reshape(yf.shape) * w.astype(jnp.float32)


def hybrid_mixer(h, cos, sin, w_in, conv_w, conv_b, dt_bias, a_log, d_skip, ssd_norm_w, sinks, w_out):
    B, S, _ = h.shape
    proj = h @ w_in
    q, k, v, z, xbc, dt_raw = jnp.split(proj, [OFF_Q, OFF_K, OFF_V, OFF_Z, OFF_XBC], axis=-1)
    G = ATT_HEADS // ATT_KV_HEADS
    q = q.reshape(B, S, ATT_KV_HEADS, G, HEAD_DIM)
    k = k.reshape(B, S, ATT_KV_HEADS, HEAD_DIM)
    v = v.reshape(B, S, ATT_KV_HEADS, HEAD_DIM)
    q = apply_partial_rope(q, cos[:, :, None, None, :], sin[:, :, None, None, :])
    k = apply_partial_rope(k, cos[:, :, None, :], sin[:, :, None, :])
    y_att = sliding_window_attention(q, k, v, sinks)
    xbc = jax.nn.silu(causal_depthwise_conv(xbc, conv_w, conv_b))
    xs, Bm, Cm = jnp.split(xbc, [SSD_WIDTH, SSD_WIDTH + SSD_GROUPS * SSD_STATE], axis=-1)
    dt = jax.nn.softplus(dt_raw.astype(jnp.float32) + dt_bias.astype(jnp.float32))
    A = -jnp.exp(a_log.astype(jnp.float32))
    xs4 = xs.astype(jnp.float32).reshape(B, S, SSD_HEADS, SSD_HEAD_DIM)
    y = ssd_chunked(xs4, dt, A,
                    Bm.astype(jnp.float32).reshape(B, S, SSD_GROUPS, SSD_STATE),
                    Cm.astype(jnp.float32).reshape(B, S, SSD_GROUPS, SSD_STATE))
    y = y + (xs4 * d_skip.astype(jnp.float32)[:, None]).reshape(B, S, SSD_WIDTH)
    y_ssd = gated_rmsnorm(y, z, ssd_norm_w).astype(h.dtype)
    return jnp.concatenate([y_att.astype(h.dtype), y_ssd], axis=-1) @ w_out


def peer_ffn(h, w_q, sub_keys1, sub_keys2, u_tab, v_tab):
    B, S, D = h.shape
    half = PEER_KEY_DIM // 2
    q = jnp.einsum('bsd,dhk->bshk', h, w_q).astype(jnp.float32)
    s1 = jnp.einsum('bshk,nk->bshn', q[..., :half], sub_keys1.astype(jnp.float32))
    s2 = jnp.einsum('bshk,nk->bshn', q[..., half:], sub_keys2.astype(jnp.float32))
    v1, i1 = lax.top_k(s1, PEER_TOPK)
    v2, i2 = lax.top_k(s2, PEER_TOPK)
    cand = (v1[..., :, None] + v2[..., None, :]).reshape(B, S, PEER_HEADS, PEER_TOPK * PEER_TOPK)
    cand_id = (i1[..., :, None] * N_KEYS + i2[..., None, :]).reshape(B, S, PEER_HEADS, PEER_TOPK * PEER_TOPK)
    top_s, pos = lax.top_k(cand, PEER_TOPK)
    ids = jnp.take_along_axis(cand_id, pos, axis=-1)
    g = jax.nn.softmax(top_s, axis=-1)
    nblk = (B * S) // PEER_BLOCK
    hb = h.reshape(nblk, PEER_BLOCK, D)
    ib = ids.reshape(nblk, PEER_BLOCK, PEER_HEADS * PEER_TOPK)
    gb = g.astype(h.dtype).reshape(nblk, PEER_BLOCK, PEER_HEADS * PEER_TOPK)

    def block(args):
        hx, ix, gx = args
        u = jnp.take(u_tab, ix, axis=0)
        act = jax.nn.gelu(jnp.einsum('td,tkd->tk', hx, u), approximate=False)
        vv = jnp.take(v_tab, ix, axis=0)
        return jnp.einsum('tk,tkd->td', gx * act, vv)

    out = lax.map(block, (hb, ib, gb))
    return out.reshape(B, S, D)


def setup_inputs(seed: int = 0) -> dict:
    key = jax.random.key(seed)
    ks = jax.random.split(key, 24)
    f32 = jnp.float32
    nrm = lambda k, shape, s: jax.random.normal(k, shape, f32) * s
    half = PEER_KEY_DIM // 2
    dt0 = jnp.exp(jax.random.uniform(ks[7], (DEPTH, SSD_HEADS), f32, math.log(1e-3), math.log(1e-1)))
    dt_bias = dt0 + jnp.log(-jnp.expm1(-dt0))
    return {
        "x": nrm(ks[0], (BATCH, SEQ, D_MODEL), 1.0),
        "c": nrm(ks[1], (BATCH, D_MODEL), 1.0),
        "positions": jnp.broadcast_to(jnp.arange(SEQ, dtype=jnp.int32), (BATCH, SEQ)),
        "w_ada": nrm(ks[2], (DEPTH, D_MODEL, N_MOD * D_MODEL), 0.1 * D_MODEL ** -0.5),
        "b_ada": nrm(ks[3], (DEPTH, N_MOD * D_MODEL), 0.01),
        "w_in": nrm(ks[4], (DEPTH, D_MODEL, IN_WIDTH), D_MODEL ** -0.5),
        "conv_w": nrm(ks[5], (DEPTH, CONV_WIDTH, XBC_WIDTH), CONV_WIDTH ** -0.5),
        "conv_b": nrm(ks[6], (DEPTH, XBC_WIDTH), 0.01),
        "dt_bias": dt_bias,
        "a_log": jnp.log(jax.random.uniform(ks[8], (DEPTH, SSD_HEADS), f32, 1.0, 16.0)),
        "d_skip": 1.0 + nrm(ks[9], (DEPTH, SSD_HEADS), 0.01),
        "ssd_norm_w": 1.0 + nrm(ks[10], (DEPTH, SSD_WIDTH), 0.01),
        "attn_sinks": nrm(ks[11], (DEPTH, ATT_HEADS), 0.5),
        "w_out": nrm(ks[12], (DEPTH, MIX_WIDTH, D_MODEL), BETA * MIX_WIDTH ** -0.5),
        "ln1_g": 1.0 + nrm(ks[13], (DEPTH, D_MODEL), 0.01),
        "ln1_b": nrm(ks[14], (DEPTH, D_MODEL), 0.01),
        "peer_wq": nrm(ks[15], (DEPTH, D_MODEL, PEER_HEADS, PEER_KEY_DIM), D_MODEL ** -0.5),
        "peer_k1": nrm(ks[16], (DEPTH, N_KEYS, half), half ** -0.5),
        "peer_k2": nrm(ks[17], (DEPTH, N_KEYS, half), half ** -0.5),
        "peer_u": nrm(ks[18], (DEPTH, N_EXPERTS, D_MODEL), D_MODEL ** -0.5),
        "peer_v": nrm(ks[19], (DEPTH, N_EXPERTS, D_MODEL), BETA),
        "ln2_g": 1.0 + nrm(ks[20], (DEPTH, D_MODEL), 0.01),
        "ln2_b": nrm(ks[21], (DEPTH, D_MODEL), 0.01),
    }


def reference(x, c, positions, w_ada, b_ada, w_in, conv_w, conv_b, dt_bias, a_log, d_skip,
              ssd_norm_w, attn_sinks, w_out, ln1_g, ln1_b, peer_wq, peer_k1, peer_k2,
              peer_u, peer_v, ln2_g, ln2_b):
    cos, sin = rope_tables(positions)
    c_act = jax.nn.silu(c)
    for l in range(DEPTH):
        mod = (c_act @ w_ada[l] + b_ada[l])[:, None, :]
        sh1, sc1, g1, sh2, sc2, g2 = jnp.split(mod, N_MOD, axis=-1)
        h = x * (1.0 + sc1) + sh1
        mix = hybrid_mixer(h, cos, sin, w_in[l], conv_w[l], conv_b[l], dt_bias[l], a_log[l],
                           d_skip[l], ssd_norm_w[l], attn_sinks[l], w_out[l])
        x = layer_norm(ALPHA * x + (1.0 + g1) * mix, ln1_g[l], ln1_b[l])
        h = x * (1.0 + sc2) + sh2
        ffn = peer_ffn(h, peer_wq[l], peer_k1[l], peer_k2[l], peer_u[l], peer_v[l])
        x = layer_norm(ALPHA * x + (1.0 + g2) * ffn, ln2_g[l], ln2_b[l])
    return x
```

```python
import functools
import math

import numpy as np
import jax
import jax.numpy as jnp
from jax import lax
from jax.experimental import pallas as pl
from jax.experimental.pallas import tpu as pltpu

F32 = jnp.float32
BF16 = jnp.bfloat16
HIGHEST = lax.Precision.HIGHEST

D_MODEL = 2048
DEPTH = 1
ATT_HEADS = 16
ATT_KV_HEADS = 2
HEAD_DIM = 64
ATT_WIDTH = ATT_HEADS * HEAD_DIM
KV_WIDTH = ATT_KV_HEADS * HEAD_DIM
WINDOW = 128
ATT_BLOCK = 128
ROT_DIM = HEAD_DIM // 4
ROPE_THETA = 500000.0
SSD_HEADS = 16
SSD_HEAD_DIM = 64
SSD_WIDTH = SSD_HEADS * SSD_HEAD_DIM
SSD_GROUPS = 2
SSD_STATE = 128
CONV_WIDTH = 4
SSD_CHUNK = 128
BC_WIDTH = SSD_GROUPS * SSD_STATE
XBC_WIDTH = SSD_WIDTH + 2 * BC_WIDTH
OFF_Q = ATT_WIDTH
OFF_K = OFF_Q + KV_WIDTH
OFF_V = OFF_K + KV_WIDTH
OFF_Z = OFF_V + SSD_WIDTH
OFF_XBC = OFF_Z + XBC_WIDTH
IN_WIDTH = OFF_XBC + SSD_HEADS
N_KEYS = 128
N_EXPERTS = N_KEYS * N_KEYS
PEER_HEADS = 8
PEER_KEY_DIM = 128
PEER_TOPK = 16
ALPHA = (2.0 * DEPTH) ** 0.25
LN_EPS = 1e-5
N_MOD = 6

LANES = 128
SUBLANES = 8
VMEM_LIMIT = 56 * 1024 * 1024

NT_DIMS = (((1,), (1,)), ((), ()))


def _cparams(*sem):
    return pltpu.CompilerParams(dimension_semantics=sem, vmem_limit_bytes=VMEM_LIMIT)


def _layer_norm(u, g, b):
    mu = jnp.mean(u, axis=-1, keepdims=True)
    d = u - mu
    var = jnp.mean(d * d, axis=-1, keepdims=True)
    return d * lax.rsqrt(var + LN_EPS) * g + b


def _silu(x):
    return x / (1.0 + jnp.exp(-x))


def _ada_kernel(c_ref, w_ref, b_ref, o_ref):
    act = _silu(c_ref[...])
    o_ref[...] = jnp.dot(act, w_ref[...], precision=HIGHEST,
                         preferred_element_type=F32) + b_ref[...]


def _ada(c8, w_ada, b_ada):
    n = w_ada.shape[1]
    tn = 1024
    return pl.pallas_call(
        _ada_kernel,
        grid=(n // tn,),
        in_specs=[pl.BlockSpec((SUBLANES, D_MODEL), lambda j: (0, 0)),
                  pl.BlockSpec((D_MODEL, tn), lambda j: (0, j)),
                  pl.BlockSpec((1, tn), lambda j: (0, j))],
        out_specs=pl.BlockSpec((SUBLANES, tn), lambda j: (0, j)),
        out_shape=jax.ShapeDtypeStruct((SUBLANES, n), F32),
        compiler_params=_cparams("arbitrary"),
        name="ada",
    )(c8, w_ada, b_ada)


def _rope(t, cosv, sinv, n):
    reps = n // LANES
    cos_t = jnp.concatenate([cosv] * reps, axis=1) if reps > 1 else cosv
    sin_t = jnp.concatenate([sinv] * reps, axis=1) if reps > 1 else sinv
    lane = lax.broadcasted_iota(jnp.int32, t.shape, 1)
    first_half = (lane % HEAD_DIM) < (ROT_DIM // 2)
    partner = jnp.where(first_half, pltpu.roll(t, n - ROT_DIM // 2, 1),
                        pltpu.roll(t, ROT_DIM // 2, 1))
    return t * cos_t + partner * sin_t


def _in_proj_kernel(tm, x_ref, pos_ref, sc_ref, sh_ref, inv_ref, sgn_ref,
                    wqkv_ref, wz_ref, wxbc_ref, wdt_ref, cw_ref, cb_ref, dtb_ref,
                    q_ref, k_ref, v_ref, z_ref, xs_ref, b_ref, c_ref, dt_ref,
                    ext_ref):
    i = pl.program_id(0)
    h = (x_ref[...] * (1.0 + sc_ref[...]) + sh_ref[...]).astype(BF16)

    qkv = jnp.dot(h, wqkv_ref[...], preferred_element_type=F32)
    ang = pos_ref[...].astype(F32) * inv_ref[...]
    cosv = jnp.cos(ang)
    sinv = jnp.sin(ang) * sgn_ref[...]
    q = _rope(qkv[:, :OFF_Q], cosv, sinv, ATT_WIDTH)
    q_ref[...] = (q * (HEAD_DIM ** -0.5)).astype(BF16)
    k_ref[...] = _rope(qkv[:, OFF_Q:OFF_K], cosv, sinv, KV_WIDTH).astype(BF16)
    v_ref[...] = qkv[:, OFF_K:OFF_V].astype(BF16)

    z_ref[...] = jnp.dot(h, wz_ref[...], preferred_element_type=F32)

    dt_raw = jnp.dot(h, wdt_ref[...], preferred_element_type=F32)
    dts = dt_raw + dtb_ref[...]
    softplus = jnp.maximum(dts, 0.0) + jnp.log(1.0 + jnp.exp(-jnp.abs(dts)))
    lane = lax.broadcasted_iota(jnp.int32, dts.shape, 1)
    dt_ref[...] = jnp.where(lane < SSD_HEADS, softplus, 0.0)

    @pl.when(i == 0)
    def _():
        ext_ref[0:SUBLANES, :] = jnp.zeros((SUBLANES, XBC_WIDTH), F32)

    ext_ref[SUBLANES:SUBLANES + tm, :] = jnp.dot(h, wxbc_ref[...],
                                                 preferred_element_type=F32)
    acc = cb_ref[...] + cw_ref[CONV_WIDTH - 1:CONV_WIDTH, :] * ext_ref[SUBLANES:SUBLANES + tm, :]
    for j in range(CONV_WIDTH - 1):
        back = CONV_WIDTH - 1 - j
        acc = acc + cw_ref[j:j + 1, :] * ext_ref[SUBLANES - back:SUBLANES - back + tm, :]
    ext_ref[0:SUBLANES, :] = ext_ref[tm:tm + SUBLANES, :]
    act = _silu(acc)
    xs_ref[...] = act[:, :SSD_WIDTH]
    b_ref[...] = act[:, SSD_WIDTH:SSD_WIDTH + BC_WIDTH]
    c_ref[...] = act[:, SSD_WIDTH + BC_WIDTH:]


def _in_proj(x, pos, sc1, sh1, inv_row, sgn_row, wqkv, wz, wxbc, wdt, conv_w, conv_b, dtb):
    s = x.shape[0]
    tm = min(512, s)
    row = lambda n: pl.BlockSpec((1, n), lambda i: (0, 0))
    full = lambda a: pl.BlockSpec(a.shape, lambda i: (0, 0))
    tile = lambda n: pl.BlockSpec((tm, n), lambda i: (i, 0))
    out_shapes = [
        jax.ShapeDtypeStruct((s, ATT_WIDTH), BF16),
        jax.ShapeDtypeStruct((s, KV_WIDTH), BF16),
        jax.ShapeDtypeStruct((s, KV_WIDTH), BF16),
        jax.ShapeDtypeStruct((s, SSD_WIDTH), F32),
        jax.ShapeDtypeStruct((s, SSD_WIDTH), F32),
        jax.ShapeDtypeStruct((s, BC_WIDTH), F32),
        jax.ShapeDtypeStruct((s, BC_WIDTH), F32),
        jax.ShapeDtypeStruct((s, LANES), F32),
    ]
    return pl.pallas_call(
        functools.partial(_in_proj_kernel, tm),
        grid=(s // tm,),
        in_specs=[tile(D_MODEL), tile(1), row(D_MODEL), row(D_MODEL), row(LANES), row(LANES),
                  full(wqkv), full(wz), full(wxbc), full(wdt),
                  full(conv_w), row(XBC_WIDTH), row(LANES)],
        out_specs=[tile(ATT_WIDTH), tile(KV_WIDTH), tile(KV_WIDTH), tile(SSD_WIDTH),
                   tile(SSD_WIDTH), tile(BC_WIDTH), tile(BC_WIDTH), tile(LANES)],
        out_shape=out_shapes,
        scratch_shapes=[pltpu.VMEM((tm + 2 * SUBLANES, XBC_WIDTH), F32)],
        compiler_params=_cparams("arbitrary"),
        name="in_proj",
    )(x, pos, sc1, sh1, inv_row, sgn_row, wqkv, wz, wxbc, wdt, conv_w, conv_b, dtb)


def _pair_blockdiag_rows(t):
    lane = lax.broadcasted_iota(jnp.int32, t.shape, 1)
    lo = lane < HEAD_DIM
    swapped = jnp.concatenate([t[:, HEAD_DIM:], t[:, :HEAD_DIM]], axis=1)
    zero = jnp.zeros_like(t)
    head0 = jnp.concatenate([jnp.where(lo, t, zero), jnp.where(lo, zero, swapped)], axis=0)
    head1 = jnp.concatenate([jnp.where(lo, swapped, zero), jnp.where(lo, zero, t)], axis=0)
    return head0, head1


def _attn_kernel(sink_ref, q_ref, kp_ref, kc_ref, vp_ref, vc_ref, o_ref):
    i = pl.program_id(0)
    blk = ATT_BLOCK
    kb = jnp.concatenate([kp_ref[...], kc_ref[...]], axis=0).astype(F32)
    vb = jnp.concatenate([vp_ref[...], vc_ref[...]], axis=0).astype(F32)
    kbd = [t.astype(BF16) for t in _pair_blockdiag_rows(kb)]
    vbd = [t.astype(BF16) for t in _pair_blockdiag_rows(vb)]

    qi = lax.broadcasted_iota(jnp.int32, (blk, 2 * blk), 0)
    kj = lax.broadcasted_iota(jnp.int32, (blk, 2 * blk), 1)
    rel = qi - kj + blk
    valid = (rel >= 0) & (rel < WINDOW) & ((kj >= blk) | (i > 0))
    group = ATT_HEADS // ATT_KV_HEADS
    for pr in range(ATT_HEADS // 2):
        kvh = (2 * pr) // group
        qp = q_ref[:, pr * LANES:(pr + 1) * LANES]
        logits = lax.dot_general(qp, kbd[kvh], NT_DIMS, preferred_element_type=F32)
        probs = []
        for hh in range(2):
            sink = sink_ref[2 * pr + hh]
            lg = jnp.where(valid, logits[:, hh * 2 * blk:(hh + 1) * 2 * blk], -jnp.inf)
            m = jnp.maximum(jnp.max(lg, axis=-1, keepdims=True), sink)
            p = jnp.exp(lg - m)
            den = jnp.sum(p, axis=-1, keepdims=True) + jnp.exp(sink - m)
            probs.append((p / den).astype(BF16))
        pcat = jnp.concatenate(probs, axis=1)
        out = jnp.dot(pcat, vbd[kvh], preferred_element_type=F32)
        o_ref[:, pr * LANES:(pr + 1) * LANES] = out.astype(BF16)


def _attn(sinks, q, k, v):
    s = q.shape[0]
    nb = s // ATT_BLOCK
    cur = lambda n: pl.BlockSpec((ATT_BLOCK, n), lambda i: (i, 0))
    prev = lambda n: pl.BlockSpec((ATT_BLOCK, n), lambda i: (jnp.maximum(i - 1, 0), 0))
    return pl.pallas_call(
        _attn_kernel,
        grid=(nb,),
        in_specs=[pl.BlockSpec(memory_space=pltpu.SMEM),
                  cur(ATT_WIDTH), prev(KV_WIDTH), cur(KV_WIDTH), prev(KV_WIDTH), cur(KV_WIDTH)],
        out_specs=cur(ATT_WIDTH),
        out_shape=jax.ShapeDtypeStruct((s, ATT_WIDTH), BF16),
        compiler_params=_cparams("arbitrary"),
        name="attn",
    )(sinks, q, k, k, v, v)


def _lane_bcast(t, col):
    return jnp.broadcast_to(t[:, col:col + 1], t.shape)


def _row_bcast(t, row):
    return jnp.broadcast_to(t[row:row + 1, :], t.shape)


def _ssd_kernel(xs_ref, b_ref, c_ref, dt_ref, z_ref, alog_ref, dskip_ref, nw_ref,
                o_ref, state_ref, y_ref):
    ci = pl.program_id(0)
    L = SSD_CHUNK
    n_pairs = SSD_HEADS // 2

    @pl.when(ci == 0)
    def _():
        state_ref[...] = jnp.zeros(state_ref.shape, F32)

    row = lax.broadcasted_iota(jnp.int32, (L, L), 0)
    col = lax.broadcasted_iota(jnp.int32, (L, L), 1)
    causal = row >= col
    row_lo = row < SSD_HEAD_DIM
    lane_lo = col < SSD_HEAD_DIM

    dt = dt_ref[...]
    a = dt * (-jnp.exp(alog_ref[...]))
    a_cum = jnp.dot(causal.astype(F32), a, precision=HIGHEST, preferred_element_type=F32)
    a_cum_t = a_cum.T
    dt_t = dt.T
    a_last = a_cum[L - 1:L, :]
    w_col = dt * jnp.exp(a_last - a_cum)
    e_cum = jnp.exp(a_cum)
    d_last = jnp.exp(_lane_bcast(a_cum_t, L - 1))

    pairs_per_group = n_pairs // SSD_GROUPS
    for g in range(SSD_GROUPS):
        bg = b_ref[:, g * SSD_STATE:(g + 1) * SSD_STATE]
        cg = c_ref[:, g * SSD_STATE:(g + 1) * SSD_STATE]
        cb = lax.dot_general(cg.astype(BF16), bg.astype(BF16), NT_DIMS,
                             preferred_element_type=F32)
        for pr in range(g * pairs_per_group, (g + 1) * pairs_per_group):
            heads = (2 * pr, 2 * pr + 1)
            xp = xs_ref[:, pr * LANES:(pr + 1) * LANES]
            zero = jnp.zeros_like(xp)

            gmat = []
            for hd in heads:
                seg = _lane_bcast(a_cum, hd) - _row_bcast(a_cum_t, hd)
                lmat = jnp.exp(jnp.where(causal, seg, -jnp.inf))
                gmat.append((cb * lmat * _row_bcast(dt_t, hd)).astype(BF16))
            lhs_d = jnp.concatenate(gmat, axis=1)
            rhs_d = jnp.concatenate([jnp.where(lane_lo, xp, zero),
                                     jnp.where(lane_lo, zero, xp)], axis=0).astype(BF16)
            y = jnp.dot(lhs_d, rhs_d, preferred_element_type=F32)

            hp = state_ref[pr]
            r_off = jnp.concatenate([jnp.where(row_lo, hp, zero),
                                     jnp.where(row_lo, zero, hp)], axis=1).astype(BF16)
            c_s = jnp.concatenate([cg * _lane_bcast(e_cum, hd) for hd in heads],
                                  axis=1).astype(BF16)
            y = y + lax.dot_general(c_s, r_off, NT_DIMS, preferred_element_type=F32)

            x_t = xp.T
            lhs_s = jnp.concatenate([jnp.where(row_lo, x_t, zero),
                                     jnp.where(row_lo, zero, x_t)], axis=1).astype(BF16)
            b_s = jnp.concatenate([bg * _lane_bcast(w_col, hd) for hd in heads],
                                  axis=0).astype(BF16)
            st = jnp.dot(lhs_s, b_s, preferred_element_type=F32)
            dec = jnp.where(row_lo, _row_bcast(d_last, heads[0]), _row_bcast(d_last, heads[1]))
            state_ref[pr] = dec * hp + st

            y_ref[:, pr * LANES:(pr + 1) * LANES] = y + xp * dskip_ref[:, pr * LANES:(pr + 1) * LANES]

    gw = SSD_WIDTH // SSD_GROUPS
    for g in range(SSD_GROUPS):
        sl = slice(g * gw, (g + 1) * gw)
        yf = y_ref[:, sl] * _silu(z_ref[:, sl])
        ms = jnp.mean(yf * yf, axis=-1, keepdims=True)
        o_ref[:, sl] = (yf * lax.rsqrt(ms + LN_EPS) * nw_ref[:, sl]).astype(BF16)


def _ssd(xs, bm, cm, dt, z, alog_row, dskip_row, nw_row):
    s = xs.shape[0]
    nc = s // SSD_CHUNK
    tile = lambda n: pl.BlockSpec((SSD_CHUNK, n), lambda i: (i, 0))
    row = lambda n: pl.BlockSpec((1, n), lambda i: (0, 0))
    return pl.pallas_call(
        _ssd_kernel,
        grid=(nc,),
        in_specs=[tile(SSD_WIDTH), tile(BC_WIDTH), tile(BC_WIDTH), tile(LANES), tile(SSD_WIDTH),
                  row(LANES), row(SSD_WIDTH), row(SSD_WIDTH)],
        out_specs=tile(SSD_WIDTH),
        out_shape=jax.ShapeDtypeStruct((s, SSD_WIDTH), BF16),
        scratch_shapes=[pltpu.VMEM((SSD_HEADS // 2, LANES, SSD_STATE), F32),
                        pltpu.VMEM((SSD_CHUNK, SSD_WIDTH), F32)],
        compiler_params=_cparams("arbitrary"),
        name="ssd",
    )(xs, bm, cm, dt, z, alog_row, dskip_row, nw_row)


def _out_proj_kernel(ya_ref, ys_ref, wa_ref, ws_ref, x_ref, g1_ref, lg_ref, lb_ref,
                     sc_ref, sh_ref, x1_ref, h2_ref):
    mix = jnp.dot(ya_ref[...], wa_ref[...], preferred_element_type=F32)
    mix = mix + jnp.dot(ys_ref[...], ws_ref[...], preferred_element_type=F32)
    x1 = _layer_norm(ALPHA * x_ref[...] + (1.0 + g1_ref[...]) * mix, lg_ref[...], lb_ref[...])
    x1_ref[...] = x1
    h2_ref[...] = (x1 * (1.0 + sc_ref[...]) + sh_ref[...]).astype(BF16)


def _out_proj(y_att, y_ssd, w_att, w_ssd, x, g1, ln_g, ln_b, sc2, sh2):
    s = x.shape[0]
    tm = min(512, s)
    tile = lambda n: pl.BlockSpec((tm, n), lambda i: (i, 0))
    row = pl.BlockSpec((1, D_MODEL), lambda i: (0, 0))
    full = lambda a: pl.BlockSpec(a.shape, lambda i: (0, 0))
    return pl.pallas_call(
        _out_proj_kernel,
        grid=(s // tm,),
        in_specs=[tile(ATT_WIDTH), tile(SSD_WIDTH), full(w_att), full(w_ssd), tile(D_MODEL),
                  row, row, row, row, row],
        out_specs=[tile(D_MODEL), tile(D_MODEL)],
        out_shape=[jax.ShapeDtypeStruct((s, D_MODEL), F32),
                   jax.ShapeDtypeStruct((s, D_MODEL), BF16)],
        compiler_params=_cparams("arbitrary"),
        name="out_proj",
    )(y_att, y_ssd, w_att, w_ssd, x, g1, ln_g, ln_b, sc2, sh2)


def _top_values(t, count):
    vals = []
    for _ in range(count):
        m = jnp.max(t, axis=0, keepdims=True)
        vals.append(m)
        t = jnp.where(t == m, -jnp.inf, t)
    return vals


def _route_kernel(h2_ref, wq_ref, k1_ref, k2_ref, s1_ref, w1_ref, s2_ref, e2_ref, tau_ref):
    half = PEER_KEY_DIM // 2
    q_t = lax.dot_general(wq_ref[...], h2_ref[...], NT_DIMS,
                          preferred_element_type=F32)
    k1 = k1_ref[...]
    k2 = k2_ref[...]
    taus = []
    for hd in range(PEER_HEADS):
        qh = q_t[hd * PEER_KEY_DIM:(hd + 1) * PEER_KEY_DIM, :]
        s1 = jnp.dot(k1, qh[:half, :], precision=HIGHEST, preferred_element_type=F32)
        s2 = jnp.dot(k2, qh[half:, :], precision=HIGHEST, preferred_element_type=F32)
        v1 = _top_values(s1, PEER_TOPK)
        v2 = _top_values(s2, PEER_TOPK)
        cands = []
        for i in range(PEER_TOPK):
            for j in range(PEER_TOPK // (i + 1)):
                cands.append(v1[i] + v2[j])
        pad = (-len(cands)) % SUBLANES
        cands += [jnp.full_like(cands[0], -jnp.inf)] * pad
        cand = jnp.concatenate(cands, axis=0)
        top = _top_values(cand, PEER_TOPK)
        m = top[0]
        zsum = jnp.zeros_like(m)
        for tv in top:
            zsum = zsum + jnp.exp(tv - m)
        taus.append(top[PEER_TOPK - 1])
        s1_ref[hd] = s1
        s2_ref[hd] = s2
        w1_ref[hd] = jnp.exp(s1 - v1[0]) / zsum
        e2_ref[hd] = jnp.exp(s2 - v2[0])
    tau_ref[...] = jnp.concatenate(taus, axis=0)


def _route(h2, wq_t, k1, k2):
    s = h2.shape[0]
    tr = min(256, s)
    full = lambda a: pl.BlockSpec(a.shape, lambda i: (0, 0))
    tok3 = pl.BlockSpec((PEER_HEADS, N_KEYS, tr), lambda i: (0, 0, i))
    arr3 = jax.ShapeDtypeStruct((PEER_HEADS, N_KEYS, s), F32)
    return pl.pallas_call(
        _route_kernel,
        grid=(s // tr,),
        in_specs=[pl.BlockSpec((tr, D_MODEL), lambda i: (i, 0)), full(wq_t), full(k1), full(k2)],
        out_specs=[tok3, tok3, tok3, tok3, pl.BlockSpec((PEER_HEADS, tr), lambda i: (0, i))],
        out_shape=[arr3, arr3, arr3, arr3, jax.ShapeDtypeStruct((PEER_HEADS, s), F32)],
        compiler_params=_cparams("arbitrary"),
        name="route",
    )(h2, wq_t, k1, k2)


def _peer_kernel(tm, eb, lc, h2_ref, u_ref, vt_ref, s1_ref, w1_ref, s2_ref, e2_ref, tau_ref,
                 x1_ref, g2_ref, lg_ref, lb_ref, o_ref, acc_ref, sc_ref, p_ref):
    j = pl.program_id(1)
    a_per_blk = eb // N_KEYS

    @pl.when(j == 0)
    def _():
        acc_ref[...] = jnp.zeros(acc_ref.shape, F32)

    sc_ref[...] = lax.dot_general(u_ref[...], h2_ref[...], NT_DIMS, preferred_element_type=F32)

    def per_a(al, carry):
        a = j * a_per_blk + al
        r0 = pl.multiple_of(al * N_KEYS, N_KEYS)
        for c in range(tm // lc):
            ls = slice(c * lc, (c + 1) * lc)
            gate = jnp.zeros((N_KEYS, lc), F32)
            for hd in range(PEER_HEADS):
                s1row = s1_ref[hd, pl.ds(a, 1), ls]
                w1row = w1_ref[hd, pl.ds(a, 1), ls]
                keep = (s2_ref[hd, :, ls] + s1row) >= tau_ref[hd:hd + 1, ls]
                gate = gate + jnp.where(keep, e2_ref[hd, :, ls], 0.0) * w1row
            sc = sc_ref[pl.ds(r0, N_KEYS), ls]
            act = 0.5 * sc * (1.0 + lax.erf(sc * (2.0 ** -0.5)))
            p_ref[pl.ds(r0, N_KEYS), ls] = (gate * act).astype(BF16)
        return carry

    lax.fori_loop(0, a_per_blk, per_a, 0)

    acc_ref[...] += jnp.dot(vt_ref[...], p_ref[...], preferred_element_type=F32)

    @pl.when(j == pl.num_programs(1) - 1)
    def _():
        ffn = acc_ref[...].T
        u = ALPHA * x1_ref[...] + (1.0 + g2_ref[...]) * ffn
        o_ref[...] = _layer_norm(u, lg_ref[...], lb_ref[...])


def _peer(h2, u_tab, vt_tab, s1, w1, s2, e2, tau, x1, g2, ln_g, ln_b):
    s = h2.shape[0]
    tm = min(512, s)
    eb = 512
    lc = 256
    tok3 = pl.BlockSpec((PEER_HEADS, N_KEYS, tm), lambda i, j: (0, 0, i))
    row = pl.BlockSpec((1, D_MODEL), lambda i, j: (0, 0))
    return pl.pallas_call(
        functools.partial(_peer_kernel, tm, eb, lc),
        grid=(s // tm, N_EXPERTS // eb),
        in_specs=[pl.BlockSpec((tm, D_MODEL), lambda i, j: (i, 0)),
                  pl.BlockSpec((eb, D_MODEL), lambda i, j: (j, 0)),
                  pl.BlockSpec((D_MODEL, eb), lambda i, j: (0, j)),
                  tok3, tok3, tok3, tok3,
                  pl.BlockSpec((PEER_HEADS, tm), lambda i, j: (0, i)),
                  pl.BlockSpec((tm, D_MODEL), lambda i, j: (i, 0)),
                  row, row, row],
        out_specs=pl.BlockSpec((tm, D_MODEL), lambda i, j: (i, 0)),
        out_shape=jax.ShapeDtypeStruct((s, D_MODEL), F32),
        scratch_shapes=[pltpu.VMEM((D_MODEL, tm), F32),
                        pltpu.VMEM((eb, tm), F32),
                        pltpu.VMEM((eb, tm), BF16)],
        compiler_params=_cparams("arbitrary", "arbitrary"),
        name="peer",
    )(h2, u_tab, vt_tab, s1, w1, s2, e2, tau, x1, g2, ln_g, ln_b)


def _rope_rows():
    j = np.arange(LANES) % HEAD_DIM
    half = ROT_DIM // 2
    inv = np.where(j < ROT_DIM, ROPE_THETA ** (-(2.0 * (j % half)) / ROT_DIM), 0.0)
    sgn = np.where(j < half, -1.0, np.where(j < ROT_DIM, 1.0, 0.0))
    return (jnp.asarray(inv, F32).reshape(1, LANES), jnp.asarray(sgn, F32).reshape(1, LANES))


def _pad_lanes(v):
    return jnp.pad(v.astype(F32), (0, LANES - v.shape[0])).reshape(1, LANES)


def kernel(x, c, positions, w_ada, b_ada, w_in, conv_w, conv_b, dt_bias, a_log, d_skip,
           ssd_norm_w, attn_sinks, w_out, ln1_g, ln1_b, peer_wq, peer_k1, peer_k2,
           peer_u, peer_v, ln2_g, ln2_b):
    assert x.shape[0] == 1 and w_ada.shape[0] == DEPTH
    seq = x.shape[1]
    xt = x[0]
    pos = positions[0].reshape(seq, 1)
    inv_row, sgn_row = _rope_rows()

    c8 = jnp.broadcast_to(c, (SUBLANES, D_MODEL))
    mod = _ada(c8, w_ada[0], b_ada[0].reshape(1, -1))[0:1]
    sh1, sc1, g1, sh2, sc2, g2 = [mod[:, k * D_MODEL:(k + 1) * D_MODEL] for k in range(N_MOD)]

    w = w_in[0]
    wqkv = w[:, :OFF_V].astype(BF16)
    wz = w[:, OFF_V:OFF_Z].astype(BF16)
    wxbc = w[:, OFF_Z:OFF_XBC].astype(BF16)
    wdt = jnp.pad(w[:, OFF_XBC:], ((0, 0), (0, LANES - SSD_HEADS))).astype(BF16)
    q, k, v, z, xs, bm, cm, dt = _in_proj(
        xt, pos, sc1, sh1, inv_row, sgn_row, wqkv, wz, wxbc, wdt,
        conv_w[0], conv_b[0].reshape(1, -1), _pad_lanes(dt_bias[0]))

    y_att = _attn(attn_sinks[0].astype(F32), q, k, v)
    y_ssd = _ssd(xs, bm, cm, dt, z, _pad_lanes(a_log[0]),
                 jnp.repeat(d_skip[0].astype(F32), SSD_HEAD_DIM).reshape(1, -1),
                 ssd_norm_w[0].reshape(1, -1))

    wo = w_out[0].astype(BF16)
    x1, h2 = _out_proj(y_att, y_ssd, wo[:ATT_WIDTH], wo[ATT_WIDTH:], xt, g1,
                       ln1_g[0].reshape(1, -1), ln1_b[0].reshape(1, -1), sc2, sh2)

    wq_t = peer_wq[0].reshape(D_MODEL, PEER_HEADS * PEER_KEY_DIM).T.astype(BF16)
    s1, w1, s2, e2, tau = _route(h2, wq_t, peer_k1[0], peer_k2[0])

    u_tab = peer_u[0].astype(BF16)
    vt_tab = peer_v[0].T.astype(BF16)
    out = _peer(h2, u_tab, vt_tab, s1, w1, s2, e2, tau, x1, g2,
                ln2_g[0].reshape(1, -1), ln2_b[0].reshape(1, -1))
    return out[None]
```

```python
import functools
import math

import numpy as np
import jax
import jax.numpy as jnp
from jax import lax
from jax.experimental import pallas as pl
from jax.experimental.pallas import tpu as pltpu

F32 = jnp.float32
BF16 = jnp.bfloat16
HIGHEST = lax.Precision.HIGHEST

D_MODEL = 2048
DEPTH = 1
ATT_HEADS = 16
ATT_KV_HEADS = 2
HEAD_DIM = 64
ATT_WIDTH = ATT_HEADS * HEAD_DIM
KV_WIDTH = ATT_KV_HEADS * HEAD_DIM
WINDOW = 128
ATT_BLOCK = 128
ROT_DIM = HEAD_DIM // 4
ROPE_THETA = 500000.0
SSD_HEADS = 16
SSD_HEAD_DIM = 64
SSD_WIDTH = SSD_HEADS * SSD_HEAD_DIM
SSD_GROUPS = 2
SSD_STATE = 128
CONV_WIDTH = 4
SSD_CHUNK = 128
BC_WIDTH = SSD_GROUPS * SSD_STATE
XBC_WIDTH = SSD_WIDTH + 2 * BC_WIDTH
OFF_Q = ATT_WIDTH
OFF_K = OFF_Q + KV_WIDTH
OFF_V = OFF_K + KV_WIDTH
OFF_Z = OFF_V + SSD_WIDTH
OFF_XBC = OFF_Z + XBC_WIDTH
IN_WIDTH = OFF_XBC + SSD_HEADS
N_KEYS = 128
N_EXPERTS = N_KEYS * N_KEYS
PEER_HEADS = 8
PEER_KEY_DIM = 128
PEER_TOPK = 16
ALPHA = (2.0 * DEPTH) ** 0.25
LN_EPS = 1e-5
N_MOD = 6

LANES = 128
SUBLANES = 8
VMEM_LIMIT = 56 * 1024 * 1024

NT_DIMS = (((1,), (1,)), ((), ()))


def _cparams(*sem):
    return pltpu.CompilerParams(dimension_semantics=sem, vmem_limit_bytes=VMEM_LIMIT)


def _layer_norm(u, g, b):
    mu = jnp.mean(u, axis=-1, keepdims=True)
    d = u - mu
    var = jnp.mean(d * d, axis=-1, keepdims=True)
    return d * lax.rsqrt(var + LN_EPS) * g + b


def _silu(x):
    return x / (1.0 + jnp.exp(-x))


def _ada_kernel(c_ref, w_ref, b_ref, o_ref):
    act = _silu(c_ref[...])
    o_ref[...] = jnp.dot(act, w_ref[...], precision=HIGHEST,
                         preferred_element_type=F32) + b_ref[...]


def _ada(c8, w_ada, b_ada):
    n = w_ada.shape[1]
    tn = 1024
    return pl.pallas_call(
        _ada_kernel,
        grid=(n // tn,),
        in_specs=[pl.BlockSpec((SUBLANES, D_MODEL), lambda j: (0, 0)),
                  pl.BlockSpec((D_MODEL, tn), lambda j: (0, j)),
                  pl.BlockSpec((1, tn), lambda j: (0, j))],
        out_specs=pl.BlockSpec((SUBLANES, tn), lambda j: (0, j)),
        out_shape=jax.ShapeDtypeStruct((SUBLANES, n), F32),
        compiler_params=_cparams("arbitrary"),
        name="ada",
    )(c8, w_ada, b_ada)


def _rope(t, cosv, sinv, n):
    reps = n // LANES
    cos_t = jnp.concatenate([cosv] * reps, axis=1) if reps > 1 else cosv
    sin_t = jnp.concatenate([sinv] * reps, axis=1) if reps > 1 else sinv
    lane = lax.broadcasted_iota(jnp.int32, t.shape, 1)
    first_half = (lane % HEAD_DIM) < (ROT_DIM // 2)
    partner = jnp.where(first_half, pltpu.roll(t, n - ROT_DIM // 2, 1),
                        pltpu.roll(t, ROT_DIM // 2, 1))
    return t * cos_t + partner * sin_t


def _in_proj_kernel(tm, x_ref, pos_ref, sc_ref, sh_ref, inv_ref, sgn_ref,
                    wqkv_ref, wz_ref, wxbc_ref, wdt_ref, cw_ref, cb_ref, dtb_ref,
                    q_ref, k_ref, v_ref, z_ref, xs_ref, b_ref, c_ref, dt_ref,
                    ext_ref):
    i = pl.program_id(0)
    h = (x_ref[...] * (1.0 + sc_ref[...]) + sh_ref[...]).astype(BF16)

    qkv = jnp.dot(h, wqkv_ref[...], preferred_element_type=F32)
    ang = pos_ref[...].astype(F32) * inv_ref[...]
    cosv = jnp.cos(ang)
    sinv = jnp.sin(ang) * sgn_ref[...]
    q = _rope(qkv[:, :OFF_Q], cosv, sinv, ATT_WIDTH)
    q_ref[...] = (q * (HEAD_DIM ** -0.5)).astype(BF16)
    k_ref[...] = _rope(qkv[:, OFF_Q:OFF_K], cosv, sinv, KV_WIDTH).astype(BF16)
    v_ref[...] = qkv[:, OFF_K:OFF_V].astype(BF16)

    z_ref[...] = jnp.dot(h, wz_ref[...], preferred_element_type=F32)

    dt_raw = jnp.dot(h, wdt_ref[...], preferred_element_type=F32)
    dts = dt_raw + dtb_ref[...]
    softplus = jnp.maximum(dts, 0.0) + jnp.log(1.0 + jnp.exp(-jnp.abs(dts)))
    lane = lax.broadcasted_iota(jnp.int32, dts.shape, 1)
    dt_ref[...] = jnp.where(lane < SSD_HEADS, softplus, 0.0)

    @pl.when(i == 0)
    def _():
        ext_ref[0:SUBLANES, :] = jnp.zeros((SUBLANES, XBC_WIDTH), F32)

    ext_ref[SUBLANES:SUBLANES + tm, :] = jnp.dot(h, wxbc_ref[...],
                                                 preferred_element_type=F32)
    acc = cb_ref[...] + cw_ref[CONV_WIDTH - 1:CONV_WIDTH, :] * ext_ref[SUBLANES:SUBLANES + tm, :]
    for j in range(CONV_WIDTH - 1):
        back = CONV_WIDTH - 1 - j
        acc = acc + cw_ref[j:j + 1, :] * ext_ref[SUBLANES - back:SUBLANES - back + tm, :]
    ext_ref[0:SUBLANES, :] = ext_ref[tm:tm + SUBLANES, :]
    act = _silu(acc)
    xs_ref[...] = act[:, :SSD_WIDTH]
    b_ref[...] = act[:, SSD_WIDTH:SSD_WIDTH + BC_WIDTH]
    c_ref[...] = act[:, SSD_WIDTH + BC_WIDTH:]


def _in_proj(x, pos, sc1, sh1, inv_row, sgn_row, wqkv, wz, wxbc, wdt, conv_w, conv_b, dtb):
    s = x.shape[0]
    tm = min(512, s)
    row = lambda n: pl.BlockSpec((1, n), lambda i: (0, 0))
    full = lambda a: pl.BlockSpec(a.shape, lambda i: (0, 0))
    tile = lambda n: pl.BlockSpec((tm, n), lambda i: (i, 0))
    out_shapes = [
        jax.ShapeDtypeStruct((s, ATT_WIDTH), BF16),
        jax.ShapeDtypeStruct((s, KV_WIDTH), BF16),
        jax.ShapeDtypeStruct((s, KV_WIDTH), BF16),
        jax.ShapeDtypeStruct((s, SSD_WIDTH), F32),
        jax.ShapeDtypeStruct((s, SSD_WIDTH), F32),
        jax.ShapeDtypeStruct((s, BC_WIDTH), F32),
        jax.ShapeDtypeStruct((s, BC_WIDTH), F32),
        jax.ShapeDtypeStruct((s, LANES), F32),
    ]
    return pl.pallas_call(
        functools.partial(_in_proj_kernel, tm),
        grid=(s // tm,),
        in_specs=[tile(D_MODEL), tile(1), row(D_MODEL), row(D_MODEL), row(LANES), row(LANES),
                  full(wqkv), full(wz), full(wxbc), full(wdt),
                  full(conv_w), row(XBC_WIDTH), row(LANES)],
        out_specs=[tile(ATT_WIDTH), tile(KV_WIDTH), tile(KV_WIDTH), tile(SSD_WIDTH),
                   tile(SSD_WIDTH), tile(BC_WIDTH), tile(BC_WIDTH), tile(LANES)],
        out_shape=out_shapes,
        scratch_shapes=[pltpu.VMEM((tm + 2 * SUBLANES, XBC_WIDTH), F32)],
        compiler_params=_cparams("arbitrary"),
        name="in_proj",
    )(x, pos, sc1, sh1, inv_row, sgn_row, wqkv, wz, wxbc, wdt, conv_w, conv_b, dtb)


def _pair_blockdiag_rows(t):
    lane = lax.broadcasted_iota(jnp.int32, t.shape, 1)
    lo = lane < HEAD_DIM
    swapped = jnp.concatenate([t[:, HEAD_DIM:], t[:, :HEAD_DIM]], axis=1)
    zero = jnp.zeros_like(t)
    head0 = jnp.concatenate([jnp.where(lo, t, zero), jnp.where(lo, zero, swapped)], axis=0)
    head1 = jnp.concatenate([jnp.where(lo, swapped, zero), jnp.where(lo, zero, t)], axis=0)
    return head0, head1


def _attn_kernel(sink_ref, q_ref, kp_ref, kc_ref, vp_ref, vc_ref, o_ref):
    i = pl.program_id(0)
    blk = ATT_BLOCK
    kb = jnp.concatenate([kp_ref[...], kc_ref[...]], axis=0).astype(F32)
    vb = jnp.concatenate([vp_ref[...], vc_ref[...]], axis=0).astype(F32)
    kbd = [t.astype(BF16) for t in _pair_blockdiag_rows(kb)]
    vbd = [t.astype(BF16) for t in _pair_blockdiag_rows(vb)]

    qi = lax.broadcasted_iota(jnp.int32, (blk, 2 * blk), 0)
    kj = lax.broadcasted_iota(jnp.int32, (blk, 2 * blk), 1)
    rel = qi - kj + blk
    valid = (rel >= 0) & (rel < WINDOW) & ((kj >= blk) | (i > 0))
    group = ATT_HEADS // ATT_KV_HEADS
    for pr in range(ATT_HEADS // 2):
        kvh = (2 * pr) // group
        qp = q_ref[:, pr * LANES:(pr + 1) * LANES]
        logits = lax.dot_general(qp, kbd[kvh], NT_DIMS, preferred_element_type=F32)
        probs = []
        for hh in range(2):
            sink = sink_ref[2 * pr + hh]
            lg = jnp.where(valid, logits[:, hh * 2 * blk:(hh + 1) * 2 * blk], -jnp.inf)
            m = jnp.maximum(jnp.max(lg, axis=-1, keepdims=True), sink)
            p = jnp.exp(lg - m)
            den = jnp.sum(p, axis=-1, keepdims=True) + jnp.exp(sink - m)
            probs.append((p / den).astype(BF16))
        pcat = jnp.concatenate(probs, axis=1)
        out = jnp.dot(pcat, vbd[kvh], preferred_element_type=F32)
        o_ref[:, pr * LANES:(pr + 1) * LANES] = out.astype(BF16)


def _attn(sinks, q, k, v):
    s = q.shape[0]
    nb = s // ATT_BLOCK
    cur = lambda n: pl.BlockSpec((ATT_BLOCK, n), lambda i: (i, 0))
    prev = lambda n: pl.BlockSpec((ATT_BLOCK, n), lambda i: (jnp.maximum(i - 1, 0), 0))
    return pl.pallas_call(
        _attn_kernel,
        grid=(nb,),
        in_specs=[pl.BlockSpec(memory_space=pltpu.SMEM),
                  cur(ATT_WIDTH), prev(KV_WIDTH), cur(KV_WIDTH), prev(KV_WIDTH), cur(KV_WIDTH)],
        out_specs=cur(ATT_WIDTH),
        out_shape=jax.ShapeDtypeStruct((s, ATT_WIDTH), BF16),
        compiler_params=_cparams("arbitrary"),
        name="attn",
    )(sinks, q, k, k, v, v)


def _lane_bcast(t, col):
    return jnp.broadcast_to(t[:, col:col + 1], t.shape)


def _row_bcast(t, row):
    return jnp.broadcast_to(t[row:row + 1, :], t.shape)


def _ssd_kernel(xs_ref, b_ref, c_ref, dt_ref, z_ref, alog_ref, dskip_ref, nw_ref,
                o_ref, state_ref, y_ref):
    ci = pl.program_id(0)
    L = SSD_CHUNK
    n_pairs = SSD_HEADS // 2

    @pl.when(ci == 0)
    def _():
        state_ref[...] = jnp.zeros(state_ref.shape, F32)

    row = lax.broadcasted_iota(jnp.int32, (L, L), 0)
    col = lax.broadcasted_iota(jnp.int32, (L, L), 1)
    causal = row >= col
    row_lo = row < SSD_HEAD_DIM
    lane_lo = col < SSD_HEAD_DIM

    dt = dt_ref[...]
    a = dt * (-jnp.exp(alog_ref[...]))
    a_cum = jnp.dot(causal.astype(F32), a, precision=HIGHEST, preferred_element_type=F32)
    a_cum_t = a_cum.T
    dt_t = dt.T
    a_last = a_cum[L - 1:L, :]
    w_col = dt * jnp.exp(a_last - a_cum)
    e_cum = jnp.exp(a_cum)
    d_last = jnp.exp(_lane_bcast(a_cum_t, L - 1))

    pairs_per_group = n_pairs // SSD_GROUPS
    for g in range(SSD_GROUPS):
        bg = b_ref[:, g * SSD_STATE:(g + 1) * SSD_STATE]
        cg = c_ref[:, g * SSD_STATE:(g + 1) * SSD_STATE]
        cb = lax.dot_general(cg.astype(BF16), bg.astype(BF16), NT_DIMS,
                             preferred_element_type=F32)
        for pr in range(g * pairs_per_group, (g + 1) * pairs_per_group):
            heads = (2 * pr, 2 * pr + 1)
            xp = xs_ref[:, pr * LANES:(pr + 1) * LANES]
            zero = jnp.zeros_like(xp)

            gmat = []
            for hd in heads:
                seg = _lane_bcast(a_cum, hd) - _row_bcast(a_cum_t, hd)
                lmat = jnp.exp(jnp.where(causal, seg, -jnp.inf))
                gmat.append((cb * lmat * _row_bcast(dt_t, hd)).astype(BF16))
            lhs_d = jnp.concatenate(gmat, axis=1)
            rhs_d = jnp.concatenate([jnp.where(lane_lo, xp, zero),
                                     jnp.where(lane_lo, zero, xp)], axis=0).astype(BF16)
            y = jnp.dot(lhs_d, rhs_d, preferred_element_type=F32)

            hp = state_ref[pr]
            r_off = jnp.concatenate([jnp.where(row_lo, hp, zero),
                                     jnp.where(row_lo, zero, hp)], axis=1).astype(BF16)
            c_s = jnp.concatenate([cg * _lane_bcast(e_cum, hd) for hd in heads],
                                  axis=1).astype(BF16)
            y = y + lax.dot_general(c_s, r_off, NT_DIMS, preferred_element_type=F32)

            x_t = xp.T
            lhs_s = jnp.concatenate([jnp.where(row_lo, x_t, zero),
                                     jnp.where(row_lo, zero, x_t)], axis=1).astype(BF16)
            b_s = jnp.concatenate([bg * _lane_bcast(w_col, hd) for hd in heads],
                                  axis=0).astype(BF16)
            st = jnp.dot(lhs_s, b_s, preferred_element_type=F32)
            dec = jnp.where(row_lo, _row_bcast(d_last, heads[0]), _row_bcast(d_last, heads[1]))
            state_ref[pr] = dec * hp + st

            y_ref[:, pr * LANES:(pr + 1) * LANES] = y + xp * dskip_ref[:, pr * LANES:(pr + 1) * LANES]

    gw = SSD_WIDTH // SSD_GROUPS
    for g in range(SSD_GROUPS):
        sl = slice(g * gw, (g + 1) * gw)
        yf = y_ref[:, sl] * _silu(z_ref[:, sl])
        ms = jnp.mean(yf * yf, axis=-1, keepdims=True)
        o_ref[:, sl] = (yf * lax.rsqrt(ms + LN_EPS) * nw_ref[:, sl]).astype(BF16)


def _ssd(xs, bm, cm, dt, z, alog_row, dskip_row, nw_row):
    s = xs.shape[0]
    nc = s // SSD_CHUNK
    tile = lambda n: pl.BlockSpec((SSD_CHUNK, n), lambda i: (i, 0))
    row = lambda n: pl.BlockSpec((1, n), lambda i: (0, 0))
    return pl.pallas_call(
        _ssd_kernel,
        grid=(nc,),
        in_specs=[tile(SSD_WIDTH), tile(BC_WIDTH), tile(BC_WIDTH), tile(LANES), tile(SSD_WIDTH),
                  row(LANES), row(SSD_WIDTH), row(SSD_WIDTH)],
        out_specs=tile(SSD_WIDTH),
        out_shape=jax.ShapeDtypeStruct((s, SSD_WIDTH), BF16),
        scratch_shapes=[pltpu.VMEM((SSD_HEADS // 2, LANES, SSD_STATE), F32),
                        pltpu.VMEM((SSD_CHUNK, SSD_WIDTH), F32)],
        compiler_params=_cparams("arbitrary"),
        name="ssd",
    )(xs, bm, cm, dt, z, alog_row, dskip_row, nw_row)


def _out_proj_kernel(ya_ref, ys_ref, wa_ref, ws_ref, x_ref, g1_ref, lg_ref, lb_ref,
                     sc_ref, sh_ref, x1_ref, h2_ref):
    mix = jnp.dot(ya_ref[...], wa_ref[...], preferred_element_type=F32)
    mix = mix + jnp.dot(ys_ref[...], ws_ref[...], preferred_element_type=F32)
    x1 = _layer_norm(ALPHA * x_ref[...] + (1.0 + g1_ref[...]) * mix, lg_ref[...], lb_ref[...])
    x1_ref[...] = x1
    h2_ref[...] = (x1 * (1.0 + sc_ref[...]) + sh_ref[...]).T.astype(BF16)


def _out_proj(y_att, y_ssd, w_att, w_ssd, x, g1, ln_g, ln_b, sc2, sh2):
    s = x.shape[0]
    tm = min(512, s)
    tile = lambda n: pl.BlockSpec((tm, n), lambda i: (i, 0))
    row = pl.BlockSpec((1, D_MODEL), lambda i: (0, 0))
    full = lambda a: pl.BlockSpec(a.shape, lambda i: (0, 0))
    return pl.pallas_call(
        _out_proj_kernel,
        grid=(s // tm,),
        in_specs=[tile(ATT_WIDTH), tile(SSD_WIDTH), full(w_att), full(w_ssd), tile(D_MODEL),
                  row, row, row, row, row],
        out_specs=[tile(D_MODEL), pl.BlockSpec((D_MODEL, tm), lambda i: (0, i))],
        out_shape=[jax.ShapeDtypeStruct((s, D_MODEL), F32),
                   jax.ShapeDtypeStruct((D_MODEL, s), BF16)],
        compiler_params=_cparams("arbitrary"),
        name="out_proj",
    )(y_att, y_ssd, w_att, w_ssd, x, g1, ln_g, ln_b, sc2, sh2)


def _top_values(t, count):
    vals = []
    rank = jnp.full(t.shape, float(count), F32)
    for k in range(count):
        m = jnp.max(t, axis=0, keepdims=True)
        vals.append(m)
        hit = t == m
        rank = jnp.where(hit, float(k), rank)
        t = jnp.where(hit, -jnp.inf, t)
    return vals, rank


def _route_kernel(h2_ref, wq_ref, k1_ref, k2_ref, jn_ref, w1_ref, r2_ref, e2_ref):
    half = PEER_KEY_DIM // 2
    q_t = jnp.dot(wq_ref[...], h2_ref[...], preferred_element_type=F32)
    k1 = k1_ref[...]
    k2 = k2_ref[...]
    for hd in range(PEER_HEADS):
        qh = q_t[hd * PEER_KEY_DIM:(hd + 1) * PEER_KEY_DIM, :]
        s1 = jnp.dot(k1, qh[:half, :], precision=HIGHEST, preferred_element_type=F32)
        s2 = jnp.dot(k2, qh[half:, :], precision=HIGHEST, preferred_element_type=F32)
        v1, _ = _top_values(s1, PEER_TOPK)
        v2, rank2 = _top_values(s2, PEER_TOPK)
        cands = []
        for i in range(PEER_TOPK):
            for j in range(PEER_TOPK // (i + 1)):
                cands.append(v1[i] + v2[j])
        pad = (-len(cands)) % SUBLANES
        cands += [jnp.full_like(cands[0], -jnp.inf)] * pad
        top, _ = _top_values(jnp.concatenate(cands, axis=0), PEER_TOPK)
        m = top[0]
        tau = top[PEER_TOPK - 1]
        zsum = jnp.zeros_like(m)
        for tv in top:
            zsum = zsum + jnp.exp(tv - m)
        count = jnp.zeros_like(s1)
        for j in range(PEER_TOPK):
            count = count + jnp.where(s1 + v2[j] >= tau, 1.0, 0.0)
        jn_ref[hd] = count
        w1_ref[hd] = jnp.exp(s1 - v1[0]) / zsum
        r2_ref[hd] = rank2.astype(BF16)
        e2_ref[hd] = jnp.exp(s2 - v2[0]).astype(BF16)


def _route(h2t, wq_t, k1, k2):
    s = h2t.shape[1]
    tr = min(256, s)
    full = lambda a: pl.BlockSpec(a.shape, lambda i: (0, 0))
    tok3 = pl.BlockSpec((PEER_HEADS, N_KEYS, tr), lambda i: (0, 0, i))
    arr3 = lambda dt: jax.ShapeDtypeStruct((PEER_HEADS, N_KEYS, s), dt)
    return pl.pallas_call(
        _route_kernel,
        grid=(s // tr,),
        in_specs=[pl.BlockSpec((D_MODEL, tr), lambda i: (0, i)), full(wq_t), full(k1), full(k2)],
        out_specs=[tok3, tok3, tok3, tok3],
        out_shape=[arr3(F32), arr3(F32), arr3(BF16), arr3(BF16)],
        compiler_params=_cparams("arbitrary"),
        name="route",
    )(h2t, wq_t, k1, k2)


def _peer_kernel(tm, eb, lc, h2t_ref, u_ref, vt_ref, jn_ref, w1_ref, r2_ref, e2_ref,
                 x1_ref, g2_ref, lg_ref, lb_ref, o_ref, acc_ref, sc_ref, p_ref):
    j = pl.program_id(1)
    a_per_blk = eb // N_KEYS
    pack = 2 * SUBLANES

    @pl.when(j == 0)
    def _():
        acc_ref[...] = jnp.zeros(acc_ref.shape, F32)

    sc_ref[...] = jnp.dot(u_ref[...], h2t_ref[...], preferred_element_type=F32)

    def lane_row(ref, hd, al, ls):
        row = jnp.broadcast_to(ref[hd, al:al + 1, ls], (pack, lc)).astype(BF16)
        return pltpu.repeat(row, N_KEYS // pack, axis=0)

    zero = jnp.zeros((N_KEYS, lc), BF16)
    for al in range(a_per_blk):
        rows = slice(al * N_KEYS, (al + 1) * N_KEYS)
        for c in range(tm // lc):
            ls = slice(c * lc, (c + 1) * lc)
            gate = zero
            for hd in range(PEER_HEADS):
                keep = r2_ref[hd, :, ls] < lane_row(jn_ref, hd, al, ls)
                gate = gate + jnp.where(keep, e2_ref[hd, :, ls], zero) * lane_row(w1_ref, hd, al, ls)
            sc = sc_ref[rows, ls]
            act = 0.5 * sc * (1.0 + lax.erf(sc * (2.0 ** -0.5)))
            p_ref[rows, ls] = gate * act.astype(BF16)

    acc_ref[...] += jnp.dot(vt_ref[...], p_ref[...], preferred_element_type=F32)

    @pl.when(j == pl.num_programs(1) - 1)
    def _():
        ffn = acc_ref[...].T
        u = ALPHA * x1_ref[...] + (1.0 + g2_ref[...]) * ffn
        o_ref[...] = _layer_norm(u, lg_ref[...], lb_ref[...])


def _peer(h2t, u_tab, vt_tab, jn, w1, r2, e2, x1, g2, ln_g, ln_b):
    s = h2t.shape[1]
    tm = min(512, s)
    eb = 1024
    lc = 256
    a_per_blk = eb // N_KEYS
    row = pl.BlockSpec((1, D_MODEL), lambda i, j: (0, 0))
    tok3 = pl.BlockSpec((PEER_HEADS, N_KEYS, tm), lambda i, j: (0, 0, i))
    key3 = pl.BlockSpec((PEER_HEADS, a_per_blk, tm), lambda i, j: (0, j, i))
    return pl.pallas_call(
        functools.partial(_peer_kernel, tm, eb, lc),
        grid=(s // tm, N_EXPERTS // eb),
        in_specs=[pl.BlockSpec((D_MODEL, tm), lambda i, j: (0, i)),
                  pl.BlockSpec((eb, D_MODEL), lambda i, j: (j, 0)),
                  pl.BlockSpec((D_MODEL, eb), lambda i, j: (0, j)),
                  key3, key3, tok3, tok3,
                  pl.BlockSpec((tm, D_MODEL), lambda i, j: (i, 0)),
                  row, row, row],
        out_specs=pl.BlockSpec((tm, D_MODEL), lambda i, j: (i, 0)),
        out_shape=jax.ShapeDtypeStruct((s, D_MODEL), F32),
        scratch_shapes=[pltpu.VMEM((D_MODEL, tm), F32),
                        pltpu.VMEM((eb, tm), F32),
                        pltpu.VMEM((eb, tm), BF16)],
        compiler_params=_cparams("arbitrary", "arbitrary"),
        name="peer",
    )(h2t, u_tab, vt_tab, jn, w1, r2, e2, x1, g2, ln_g, ln_b)


def _rope_rows():
    j = np.arange(LANES) % HEAD_DIM
    half = ROT_DIM // 2
    inv = np.where(j < ROT_DIM, ROPE_THETA ** (-(2.0 * (j % half)) / ROT_DIM), 0.0)
    sgn = np.where(j < half, -1.0, np.where(j < ROT_DIM, 1.0, 0.0))
    return (jnp.asarray(inv, F32).reshape(1, LANES), jnp.asarray(sgn, F32).reshape(1, LANES))


def _pad_lanes(v):
    return jnp.pad(v.astype(F32), (0, LANES - v.shape[0])).reshape(1, LANES)


def kernel(x, c, positions, w_ada, b_ada, w_in, conv_w, conv_b, dt_bias, a_log, d_skip,
           ssd_norm_w, attn_sinks, w_out, ln1_g, ln1_b, peer_wq, peer_k1, peer_k2,
           peer_u, peer_v, ln2_g, ln2_b):
    assert x.shape[0] == 1 and w_ada.shape[0] == DEPTH
    seq = x.shape[1]
    xt = x[0]
    pos = positions[0].reshape(seq, 1)
    inv_row, sgn_row = _rope_rows()

    c8 = jnp.broadcast_to(c, (SUBLANES, D_MODEL))
    mod = _ada(c8, w_ada[0], b_ada[0].reshape(1, -1))[0:1]
    sh1, sc1, g1, sh2, sc2, g2 = [mod[:, k * D_MODEL:(k + 1) * D_MODEL] for k in range(N_MOD)]

    w = w_in[0]
    wqkv = w[:, :OFF_V].astype(BF16)
    wz = w[:, OFF_V:OFF_Z].astype(BF16)
    wxbc = w[:, OFF_Z:OFF_XBC].astype(BF16)
    wdt = jnp.pad(w[:, OFF_XBC:], ((0, 0), (0, LANES - SSD_HEADS))).astype(BF16)
    q, k, v, z, xs, bm, cm, dt = _in_proj(
        xt, pos, sc1, sh1, inv_row, sgn_row, wqkv, wz, wxbc, wdt,
        conv_w[0], conv_b[0].reshape(1, -1), _pad_lanes(dt_bias[0]))

    y_att = _attn(attn_sinks[0].astype(F32), q, k, v)
    y_ssd = _ssd(xs, bm, cm, dt, z, _pad_lanes(a_log[0]),
                 jnp.repeat(d_skip[0].astype(F32), SSD_HEAD_DIM).reshape(1, -1),
                 ssd_norm_w[0].reshape(1, -1))

    wo = w_out[0].astype(BF16)
    x1, h2t = _out_proj(y_att, y_ssd, wo[:ATT_WIDTH], wo[ATT_WIDTH:], xt, g1,
                       ln1_g[0].reshape(1, -1), ln1_b[0].reshape(1, -1), sc2, sh2)

    wq_t = peer_wq[0].reshape(D_MODEL, PEER_HEADS * PEER_KEY_DIM).T.astype(BF16)
    jn, w1, r2, e2 = _route(h2t, wq_t, peer_k1[0], peer_k2[0])

    u_tab = peer_u[0].astype(BF16)
    vt_tab = peer_v[0].T.astype(BF16)
    out = _peer(h2t, u_tab, vt_tab, jn, w1, r2, e2, x1, g2,
                ln2_g[0].reshape(1, -1), ln2_b[0].reshape(1, -1))
    return out[None]
```
